```python
import math
import jax, jax.numpy as jnp
from jax import lax
import numpy as np

D_MODEL = 1024
BATCH = 8
SEQ = 2048
DEPTH = 1
DEC_BATCH = 128
DEC_SEQ = 4
PAST_LEN = 16384
PAGE_SIZE = 128

MIX_WIDTH = D_MODEL
DN_WIDTH = MIX_WIDTH // 2
DN_HEADS = 4
DN_HEAD_DIM = DN_WIDTH // DN_HEADS
GLA_WIDTH = MIX_WIDTH - DN_WIDTH
GLA_HEADS = 4
GLA_HEAD_DIM = GLA_WIDTH // GLA_HEADS
GLA_GATE_RANK = 16
GLA_GATE_NORM = 16.0
CONV_WIDTH = 4
DN_CHUNK = 64
GLA_CHUNK = 16
D_FF = 4 * D_MODEL
EPS = 1e-6
IN_SIZES = (3 * DN_WIDTH, DN_WIDTH, DN_HEADS, DN_HEADS, GLA_WIDTH, GLA_WIDTH, GLA_WIDTH, GLA_WIDTH, GLA_GATE_RANK)
IN_COLS = 3 * DN_WIDTH + DN_WIDTH + 2 * DN_HEADS + 4 * GLA_WIDTH + GLA_GATE_RANK

kernel_name = "hymba_gdn_gla_adaln_step"


def rms_norm(x, w):
    xf = x.astype(jnp.float32)
    y = xf * lax.rsqrt(jnp.mean(xf * xf, axis=-1, keepdims=True) + EPS)
    return (y * w.astype(jnp.float32)).astype(x.dtype)


def l2norm(x):
    return x * lax.rsqrt(jnp.sum(x * x, axis=-1, keepdims=True) + EPS)


def gated_head_norm(o, gate, w):
    o = o * lax.rsqrt(jnp.mean(o * o, axis=-1, keepdims=True) + EPS)
    return o * w.astype(jnp.float32) * jax.nn.silu(gate.astype(jnp.float32))


def short_conv(u, buf, w):
    L = u.shape[1]
    up = jnp.concatenate([buf.astype(u.dtype), u], axis=1)
    out = up[:, 0:L] * w[0]
    for i in range(1, CONV_WIDTH):
        out = out + up[:, i:i + L] * w[i]
    return jax.nn.silu(out), up[:, up.shape[1] - (CONV_WIDTH - 1):]


def _pad_time(t, Lp):
    if Lp == t.shape[1]:
        return t
    widths = [(0, 0)] * t.ndim
    widths[1] = (0, Lp - t.shape[1])
    return jnp.pad(t, widths)


def _to_chunks(t, C):
    B, Lp, H = t.shape[:3]
    t = t.reshape((B, Lp // C, C, H) + t.shape[3:])
    return jnp.swapaxes(t, 2, 3)


def _from_chunks(t, L):
    B, N, H, C, d = t.shape
    return jnp.swapaxes(t, 2, 3).reshape(B, N * C, H, d)[:, :L]


def gated_delta_rule(q, k, v, g, beta, s0):
    B, L, H, dk = k.shape
    dv = v.shape[-1]
    C = min(DN_CHUNK, L)
    Lp = -(-L // C) * C
    q, k, v = (_to_chunks(_pad_time(t, Lp), C) for t in (q, k, v))
    g, beta = (_to_chunks(_pad_time(t, Lp)[..., None], C)[..., 0] for t in (g, beta))
    G = jnp.cumsum(g, axis=-1)
    idx = jnp.arange(C)
    incl = idx[:, None] >= idx[None, :]
    strict = idx[:, None] > idx[None, :]
    decay = jnp.exp(jnp.where(incl, G[..., :, None] - G[..., None, :], -jnp.inf))
    kb = k * beta[..., None]
    lower = jnp.where(strict, jnp.einsum('bnhid,bnhjd->bnhij', kb, k) * decay, 0.0)
    tmat = lower + jnp.eye(C, dtype=lower.dtype)
    rhs = jnp.concatenate([v * beta[..., None], kb * jnp.exp(G)[..., None]], axis=-1)
    sol = lax.linalg.triangular_solve(tmat, rhs, left_side=True, lower=True, unit_diagonal=True)
    w_val, w_key = sol[..., :dv], sol[..., dv:]
    attn = jnp.einsum('bnhid,bnhjd->bnhij', q, k) * decay
    q_dec = q * jnp.exp(G)[..., None]
    k_dec = k * jnp.exp(G[..., -1:] - G)[..., None]
    g_last = jnp.exp(G[..., -1])

    def step(S, xs):
        wv, wk, a, qd, kd, gl = xs
        u = wv - jnp.einsum('bhck,bhkv->bhcv', wk, S)
        o = jnp.einsum('bhck,bhkv->bhcv', qd, S) + jnp.einsum('bhij,bhjv->bhiv', a, u)
        S = S * gl[..., None, None] + jnp.einsum('bhck,bhcv->bhkv', kd, u)
        return S, o

    xs = tuple(jnp.moveaxis(t, 1, 0) for t in (w_val, w_key, attn, q_dec, k_dec, g_last))
    S, o = lax.scan(step, s0, xs)
    return _from_chunks(jnp.moveaxis(o, 0, 1), L), S


def gla_recurrence(q, k, v, glog, s0):
    B, L, H, dk = k.shape
    C = min(GLA_CHUNK, L)
    Lp = -(-L // C) * C
    q, k, v, glog = (_to_chunks(_pad_time(t, Lp), C) for t in (q, k, v, glog))
    Bc = jnp.cumsum(glog, axis=3)
    q_dec = q * jnp.exp(Bc)
    k_inv = k * jnp.exp(-Bc)
    idx = jnp.arange(C)
    incl = idx[:, None] >= idx[None, :]
    attn = jnp.where(incl, jnp.einsum('bnhik,bnhjk->bnhij', q_dec, k_inv), 0.0)
    o_intra = jnp.einsum('bnhij,bnhjv->bnhiv', attn, v)
    k_dec = k * jnp.exp(Bc[..., -1:, :] - Bc)
    g_last = jnp.exp(Bc[..., -1, :])

    def step(S, xs):
        qd, kd, vv, gl = xs
        o = jnp.einsum('bhck,bhkv->bhcv', qd, S)
        S = S * gl[..., :, None] + jnp.einsum('bhck,bhcv->bhkv', kd, vv)
        return S, o

    xs = tuple(jnp.moveaxis(t, 1, 0) for t in (q_dec, k_dec, v, g_last))
    S, o_inter = lax.scan(step, s0, xs)
    o = o_intra + jnp.moveaxis(o_inter, 0, 1)
    return _from_chunks(o, L), S


def hybrid_mixer(h, conv_buf, s_dn, s_gla, w_in, w_conv, dn_a_log, dn_dt_bias, w_dn_norm,
                 w_gla_g2, b_gla_g, w_gla_norm, w_o):
    B, L, _ = h.shape
    f32 = jnp.float32
    proj = h @ w_in
    cuts = []
    acc = 0
    for s in IN_SIZES[:-1]:
        acc += s
        cuts.append(acc)
    dn_qkv, dn_z, dn_a, dn_b, gq, gk, gv, gr, gg = jnp.split(proj, cuts, axis=-1)

    qkv, conv_new = short_conv(dn_qkv, conv_buf, w_conv)
    dq, dk_, dv_ = jnp.split(qkv.astype(f32), 3, axis=-1)
    dq = l2norm(dq.reshape(B, L, DN_HEADS, DN_HEAD_DIM)) * (DN_HEAD_DIM ** -0.5)
    dk_ = l2norm(dk_.reshape(B, L, DN_HEADS, DN_HEAD_DIM))
    dv_ = dv_.reshape(B, L, DN_HEADS, DN_HEAD_DIM)
    beta = jax.nn.sigmoid(dn_b.astype(f32))
    g = -jnp.exp(dn_a_log.astype(f32)) * jax.nn.softplus(dn_a.astype(f32) + dn_dt_bias.astype(f32))
    o_dn, s_dn_new = gated_delta_rule(dq, dk_, dv_, g, beta, s_dn.astype(f32))
    o_dn = gated_head_norm(o_dn, dn_z.reshape(B, L, DN_HEADS, DN_HEAD_DIM), w_dn_norm).reshape(B, L, DN_WIDTH)

    glog = jax.nn.log_sigmoid((gg @ w_gla_g2 + b_gla_g).astype(f32)) / GLA_GATE_NORM
    glog = glog.reshape(B, L, GLA_HEADS, GLA_HEAD_DIM)
    q_g = gq.astype(f32).reshape(B, L, GLA_HEADS, GLA_HEAD_DIM) * (GLA_HEAD_DIM ** -0.5)
    k_g = gk.astype(f32).reshape(B, L, GLA_HEADS, GLA_HEAD_DIM)
    v_g = gv.astype(f32).reshape(B, L, GLA_HEADS, GLA_HEAD_DIM)
    o_gla, s_gla_new = gla_recurrence(q_g, k_g, v_g, glog, s_gla.astype(f32))
    o_gla = gated_head_norm(o_gla, gr.reshape(B, L, GLA_HEADS, GLA_HEAD_DIM), w_gla_norm).reshape(B, L, GLA_WIDTH)

    out = jnp.concatenate([o_dn, o_gla], axis=-1).astype(h.dtype) @ w_o
    return out, conv_new, s_dn_new, s_gla_new


def trunk(x, c, conv_bufs, s_dns, s_glas, w_ada, b_ada, w_norm1, w_in, w_conv, dn_a_log, dn_dt_bias,
          w_dn_norm, w_gla_g2, b_gla_g, w_gla_norm, w_o, w_norm2, w_up, w_down, w_norm_f):
    new_conv, new_dn, new_gla = [], [], []
    cs = jax.nn.silu(c.astype(x.dtype))
    for layer in range(DEPTH):
        mod = (cs @ w_ada[layer] + b_ada[layer])[:, None, :]
        sh1, sc1, gt1, sh2, sc2, gt2 = jnp.split(mod, 6, axis=-1)
        h = rms_norm(x, w_norm1[layer]) * (1 + sc1) + sh1
        mix, cb, sd, sg = hybrid_mixer(h, conv_bufs[layer], s_dns[layer], s_glas[layer], w_in[layer],
                                       w_conv[layer], dn_a_log[layer], dn_dt_bias[layer], w_dn_norm[layer],
                                       w_gla_g2[layer], b_gla_g[layer], w_gla_norm[layer], w_o[layer])
        x = x + gt1 * mix
        h = rms_norm(x, w_norm2[layer]) * (1 + sc2) + sh2
        x = x + gt2 * (jnp.square(jax.nn.relu(h @ w_up[layer])) @ w_down[layer])
        new_conv.append(cb)
        new_dn.append(sd)
        new_gla.append(sg)
    y = rms_norm(x, w_norm_f)
    return (y, jnp.stack(new_conv).astype(x.dtype), jnp.stack(new_dn).astype(x.dtype),
            jnp.stack(new_gla).astype(x.dtype))


def setup_inputs(seed: int = 0) -> dict:
    key = jax.random.key(seed)
    ks = jax.random.split(key, 24)
    nrm = jax.random.normal
    dt = jnp.exp(jax.random.uniform(ks[10], (DEPTH, DN_HEADS)) * (math.log(0.1) - math.log(0.001)) + math.log(0.001))
    return {
        "x_prompt": nrm(ks[0], (BATCH, SEQ, D_MODEL), jnp.float32),
        "x_sample": nrm(ks[1], (DEC_BATCH, DEC_SEQ, D_MODEL), jnp.float32),
        "state_dn_conv": nrm(ks[2], (DEPTH, DEC_BATCH, CONV_WIDTH - 1, 3 * DN_WIDTH), jnp.float32),
        "state_dn": 0.5 * nrm(ks[3], (DEPTH, DEC_BATCH, DN_HEADS, DN_HEAD_DIM, DN_HEAD_DIM), jnp.float32),
        "state_gla": 0.5 * nrm(ks[4], (DEPTH, DEC_BATCH, GLA_HEADS, GLA_HEAD_DIM, GLA_HEAD_DIM), jnp.float32),
        "c_prompt": nrm(ks[5], (BATCH, D_MODEL), jnp.float32),
        "c_sample": nrm(ks[6], (DEC_BATCH, D_MODEL), jnp.float32),
        "w_ada": 0.5 * D_MODEL ** -0.5 * nrm(ks[7], (DEPTH, D_MODEL, 6 * D_MODEL), jnp.float32),
        "b_ada": 0.01 * nrm(ks[8], (DEPTH, 6 * D_MODEL), jnp.float32),
        "w_norm1": 1.0 + 0.02 * nrm(ks[9], (DEPTH, D_MODEL), jnp.float32),
        "w_in": D_MODEL ** -0.5 * nrm(ks[11], (DEPTH, D_MODEL, IN_COLS), jnp.float32),
        "w_conv": CONV_WIDTH ** -0.5 * nrm(ks[12], (DEPTH, CONV_WIDTH, 3 * DN_WIDTH), jnp.float32),
        "dn_a_log": jnp.log(jax.random.uniform(ks[13], (DEPTH, DN_HEADS), jnp.float32, 1.0, 16.0)),
        "dn_dt_bias": dt + jnp.log(-jnp.expm1(-dt)),
        "w_dn_norm": 1.0 + 0.02 * nrm(ks[14], (DEPTH, DN_HEAD_DIM), jnp.float32),
        "w_gla_g2": GLA_GATE_RANK ** -0.5 * nrm(ks[15], (DEPTH, GLA_GATE_RANK, GLA_WIDTH), jnp.float32),
        "b_gla_g": 0.1 * nrm(ks[16], (DEPTH, GLA_WIDTH), jnp.float32),
        "w_gla_norm": 1.0 + 0.02 * nrm(ks[17], (DEPTH, GLA_HEAD_DIM), jnp.float32),
        "w_o": MIX_WIDTH ** -0.5 * nrm(ks[18], (DEPTH, MIX_WIDTH, D_MODEL), jnp.float32),
        "w_norm2": 1.0 + 0.02 * nrm(ks[19], (DEPTH, D_MODEL), jnp.float32),
        "w_up": D_MODEL ** -0.5 * nrm(ks[20], (DEPTH, D_MODEL, D_FF), jnp.float32),
        "w_down": D_FF ** -0.5 * nrm(ks[21], (DEPTH, D_FF, D_MODEL), jnp.float32),
        "w_norm_f": 1.0 + 0.02 * nrm(ks[22], (D_MODEL,), jnp.float32),
    }


def reference(x_prompt, x_sample, state_dn_conv, state_dn, state_gla, c_prompt, c_sample, w_ada, b_ada,
              w_norm1, w_in, w_conv, dn_a_log, dn_dt_bias, w_dn_norm, w_gla_g2, b_gla_g, w_gla_norm, w_o,
              w_norm2, w_up, w_down, w_norm_f):
    params = (w_ada, b_ada, w_norm1, w_in, w_conv, dn_a_log, dn_dt_bias, w_dn_norm, w_gla_g2, b_gla_g,
              w_gla_norm, w_o, w_norm2, w_up, w_down, w_norm_f)
    nb = x_prompt.shape[0]
    zero_conv = jnp.zeros((DEPTH, nb, CONV_WIDTH - 1, 3 * DN_WIDTH), x_prompt.dtype)
    zero_dn = jnp.zeros((DEPTH, nb, DN_HEADS, DN_HEAD_DIM, DN_HEAD_DIM), jnp.float32)
    zero_gla = jnp.zeros((DEPTH, nb, GLA_HEADS, GLA_HEAD_DIM, GLA_HEAD_DIM), jnp.float32)
    y_prompt, conv_p, dn_p, gla_p = trunk(x_prompt, c_prompt, zero_conv, zero_dn, zero_gla, *params)
    y_sample, conv_s, dn_s, gla_s = trunk(x_sample, c_sample, state_dn_conv, state_dn, state_gla, *params)
    return (y_prompt, y_sample, conv_p, dn_p, gla_p, conv_s, dn_s, gla_s)
```

```python
import functools

import jax
import jax.numpy as jnp
from jax import lax
from jax.experimental import pallas as pl
from jax.experimental.pallas import tpu as pltpu

F32 = jnp.float32
BF16 = jnp.bfloat16
HI = lax.Precision.HIGHEST

D_MODEL = 1024
N_HEADS = 4
HEAD_DIM = 128
MIX_HALF = N_HEADS * HEAD_DIM
QKV_W = 3 * MIX_HALF
GATE_RANK = 16
GATE_NORM = 16.0
CONV_W = 4
D_FF = 4 * D_MODEL
EPS = 1e-6
MAIN_COLS = 4096
SMALL_COLS = 128
GLA_SUB = 16
SUBLANES = 8

VMEM_LIMIT = 56 * 1024 * 1024


def _dot(a, b, precision=None):
    return jnp.dot(a, b, preferred_element_type=F32, precision=precision)


def _dot_nt(a, b, precision=None):
    return lax.dot_general(a, b, (((1,), (1,)), ((), ())), preferred_element_type=F32, precision=precision)


def _dot_tn(a, b, precision=None):
    return lax.dot_general(a, b, (((0,), (0,)), ((), ())), preferred_element_type=F32, precision=precision)


def _sigmoid(x):
    return 1.0 / (1.0 + jnp.exp(-x))


def _silu(x):
    return x * _sigmoid(x)


def _softplus(x):
    return jnp.maximum(x, 0.0) + jnp.log(1.0 + jnp.exp(-jnp.abs(x)))


def _rms(x, w):
    return x * lax.rsqrt(jnp.mean(x * x, axis=-1, keepdims=True) + EPS) * w


def _adaln_kernel(c_ref, w_ref, b_ref, o_ref):
    cs = _silu(c_ref[...]).astype(BF16)
    o_ref[...] = _dot(cs, w_ref[...].astype(BF16)) + b_ref[...]


def _adaln(c, w_ada, b_ada):
    n = c.shape[0]
    nblk = w_ada.shape[1] // D_MODEL
    return pl.pallas_call(
        _adaln_kernel,
        grid=(nblk,),
        in_specs=[
            pl.BlockSpec((n, D_MODEL), lambda j: (0, 0)),
            pl.BlockSpec((D_MODEL, D_MODEL), lambda j: (0, j)),
            pl.BlockSpec((1, D_MODEL), lambda j: (0, j)),
        ],
        out_specs=pl.BlockSpec((n, D_MODEL), lambda j: (0, j)),
        out_shape=jax.ShapeDtypeStruct((n, w_ada.shape[1]), F32),
        compiler_params=pltpu.CompilerParams(vmem_limit_bytes=VMEM_LIMIT),
        name="adaln",
    )(c, w_ada, b_ada)


def _inproj_kernel(x_ref, sh_ref, sc_ref, wn_ref, wmain_ref, wsmall_ref, wg2_ref, bg_ref,
                   qkv_ref, z_ref, gla_ref, small_ref, glog_ref):
    h = _rms(x_ref[...], wn_ref[...]) * (1.0 + sc_ref[0]) + sh_ref[0]
    hb = h.astype(BF16)
    main = _dot(hb, wmain_ref[...])
    qkv_ref[...] = main[:, :QKV_W]
    z_ref[...] = main[:, QKV_W:QKV_W + MIX_HALF]
    gla_ref[...] = main[:, QKV_W + MIX_HALF:]
    small = _dot(hb, wsmall_ref[...])
    small_ref[...] = small
    pre = _dot(small.astype(BF16), wg2_ref[...]) + bg_ref[...]
    log_sig = jnp.minimum(pre, 0.0) - jnp.log(1.0 + jnp.exp(-jnp.abs(pre)))
    glog_ref[...] = log_sig / GATE_NORM


def _inproj(x2d, mod3, tiles_per_mod, tm, wn1, wmain, wsmall, wg2, bg):
    t_rows = x2d.shape[0]
    r = mod3.shape[1]
    grid = (t_rows // tm,)
    row = lambda w: pl.BlockSpec((tm, w), lambda i: (i, 0))
    const = lambda shape: pl.BlockSpec(shape, lambda i: (0,) * len(shape), pipeline_mode=pl.Buffered(1))
    modspec = lambda j: pl.BlockSpec((1, r, D_MODEL), lambda i: (i // tiles_per_mod, 0, j))
    return pl.pallas_call(
        _inproj_kernel,
        grid=grid,
        in_specs=[row(D_MODEL), modspec(0), modspec(1), const((1, D_MODEL)),
                  const((D_MODEL, MAIN_COLS)), const((D_MODEL, SMALL_COLS)),
                  const((SMALL_COLS, MIX_HALF)), const((1, MIX_HALF))],
        out_specs=[row(QKV_W), row(MIX_HALF), row(4 * MIX_HALF), row(SMALL_COLS), row(MIX_HALF)],
        out_shape=[jax.ShapeDtypeStruct((t_rows, w), F32)
                   for w in (QKV_W, MIX_HALF, 4 * MIX_HALF, SMALL_COLS, MIX_HALF)],
        compiler_params=pltpu.CompilerParams(vmem_limit_bytes=VMEM_LIMIT),
        name="inproj",
    )(x2d, mod3, mod3, wn1, wmain, wsmall, wg2, bg)


def _inv_unit_lower(a, c):
    ri = lax.broadcasted_iota(jnp.int32, (c, c), 0)
    ci = lax.broadcasted_iota(jnp.int32, (c, c), 1)
    x = jnp.where(ri == ci, 1.0, 0.0) - a
    p = _dot(a, a, HI)
    n = 2
    while n < c:
        x = x + _dot(x, p, HI)
        n *= 2
        if n < c:
            p = _dot(p, p, HI)
    return x


def _dn_kernel(qkv_ref, small_ref, z_ref, conv0_ref, s0_ref, wconv_ref, alog_ref, dtb_ref, wnorm_ref,
               o_ref, s_out_ref, tail_ref, s_ref, *, chunk, l_real):
    c = chunk
    i = pl.program_id(1)

    @pl.when(i == 0)
    def _():
        tail_ref[...] = conv0_ref[0]
        s_ref[...] = s0_ref[0]

    u = qkv_ref[...]
    ext = jnp.concatenate([tail_ref[...], u], axis=0)
    w = wconv_ref[...]
    acc = u * w[CONV_W - 1:CONV_W]
    for s in range(1, CONV_W):
        shifted = pltpu.roll(ext, s, 0)[SUBLANES:SUBLANES + c]
        acc = acc + shifted * w[CONV_W - 1 - s:CONV_W - s]
    tail_ref[...] = ext[c:c + SUBLANES]
    act = _silu(acc)

    rows = lax.broadcasted_iota(jnp.int32, (c, 1), 0)
    valid = rows < l_real
    small = small_ref[...]
    g_all = -jnp.exp(alog_ref[...]) * _softplus(small + dtb_ref[...])
    g_all = jnp.where(valid, g_all, 0.0)
    beta_all = jnp.where(valid, _sigmoid(small), 0.0)

    ri = lax.broadcasted_iota(jnp.int32, (c, c), 0)
    ci = lax.broadcasted_iota(jnp.int32, (c, c), 1)
    ltri = jnp.where(ri >= ci, 1.0, 0.0)
    g_cum = _dot(ltri, g_all, HI)
    g_cum_t = g_cum.T
    z = z_ref[...]

    for h in range(N_HEADS):
        lo, hi = h * HEAD_DIM, (h + 1) * HEAD_DIM
        gc = g_cum[:, h:h + 1]
        gr = g_cum_t[h:h + 1, :]
        decay = jnp.where(ri >= ci, jnp.exp(jnp.minimum(gc - gr, 0.0)), 0.0)
        beta = beta_all[:, N_HEADS + h:N_HEADS + h + 1]
        q = act[:, lo:hi]
        k = act[:, MIX_HALF + lo:MIX_HALF + hi]
        v = act[:, 2 * MIX_HALF + lo:2 * MIX_HALF + hi]
        q = q * lax.rsqrt(jnp.sum(q * q, axis=-1, keepdims=True) + EPS) * (HEAD_DIM ** -0.5)
        k = k * lax.rsqrt(jnp.sum(k * k, axis=-1, keepdims=True) + EPS)
        k = jnp.where(valid, k, 0.0)
        kb = k * beta
        eg = jnp.exp(gc)
        kbf = k.astype(BF16)
        a = jnp.where(ri > ci, _dot_nt(kb.astype(BF16), kbf) * decay, 0.0)
        t_inv = _inv_unit_lower(a, c)
        rhs = jnp.concatenate([v * beta, kb * eg], axis=1)
        sol = _dot(t_inv, rhs, HI)
        w_val, w_key = sol[:, :HEAD_DIM], sol[:, HEAD_DIM:]
        attn = _dot_nt(q.astype(BF16), kbf) * decay
        s_h = s_ref[h]
        s_bf = s_h.astype(BF16)
        u_new = w_val - _dot(w_key.astype(BF16), s_bf)
        u_bf = u_new.astype(BF16)
        o = _dot((q * eg).astype(BF16), s_bf) + _dot(attn.astype(BF16), u_bf)
        g_last = g_cum[c - 1:c, h:h + 1]
        k_dec = k * jnp.exp(g_last - gc)
        s_ref[h] = s_h * jnp.exp(g_last) + _dot_tn(k_dec.astype(BF16), u_bf)
        o = o * lax.rsqrt(jnp.mean(o * o, axis=-1, keepdims=True) + EPS)
        o_ref[:, lo:hi] = o * wnorm_ref[...] * _silu(z[:, lo:hi])

    @pl.when(i == pl.num_programs(1) - 1)
    def _():
        s_out_ref[0] = s_ref[...]


def _dn_mixer(qkv, small, z, conv0, s0, wconv8, alog, dtb, wnorm, *, nb, nchunks, chunk, l_real):
    rows = lambda w: pl.BlockSpec((chunk, w), lambda b, i: (b * nchunks + i, 0))
    const = lambda shape: pl.BlockSpec(shape, lambda b, i: (0,) * len(shape))
    state = pl.BlockSpec((1, N_HEADS, HEAD_DIM, HEAD_DIM), lambda b, i: (b, 0, 0, 0))
    return pl.pallas_call(
        functools.partial(_dn_kernel, chunk=chunk, l_real=l_real),
        grid=(nb, nchunks),
        in_specs=[rows(QKV_W), rows(SMALL_COLS), rows(MIX_HALF),
                  pl.BlockSpec((1, SUBLANES, QKV_W), lambda b, i: (b, 0, 0)), state,
                  const((SUBLANES, QKV_W)), const((1, SMALL_COLS)), const((1, SMALL_COLS)),
                  const((1, HEAD_DIM))],
        out_specs=[rows(MIX_HALF), state],
        out_shape=[jax.ShapeDtypeStruct((nb * nchunks * chunk, MIX_HALF), F32),
                   jax.ShapeDtypeStruct((nb, N_HEADS, HEAD_DIM, HEAD_DIM), F32)],
        scratch_shapes=[pltpu.VMEM((SUBLANES, QKV_W), F32),
                        pltpu.VMEM((N_HEADS, HEAD_DIM, HEAD_DIM), F32)],
        compiler_params=pltpu.CompilerParams(dimension_semantics=("arbitrary", "arbitrary"),
                                             vmem_limit_bytes=VMEM_LIMIT),
        name="dn_mixer",
    )(qkv, small, z, conv0, s0, wconv8, alog, dtb, wnorm)


def _gla_kernel(gla_ref, glog_ref, s0_ref, wnorm_ref, o_ref, s_out_ref, st_ref, *, chunk, l_real):
    c = chunk
    nsb = c // GLA_SUB
    i = pl.program_id(1)

    @pl.when(i == 0)
    def _():
        for h in range(N_HEADS):
            st_ref[h] = s0_ref[0, h].T

    rows = lax.broadcasted_iota(jnp.int32, (c, 1), 0)
    valid = rows < l_real
    ri = lax.broadcasted_iota(jnp.int32, (c, c), 0)
    ci = lax.broadcasted_iota(jnp.int32, (c, c), 1)
    ltri = jnp.where(ri >= ci, 1.0, 0.0)
    glog = jnp.where(valid, glog_ref[...], 0.0)
    b_cum = _dot(ltri, glog, HI)
    sb_start = (ri // GLA_SUB) * GLA_SUB
    r_ref = _dot(jnp.where(ci < sb_start, 1.0, 0.0), glog, HI)
    same_sb = (ri // GLA_SUB) == (ci // GLA_SUB)
    row_sb = rows // GLA_SUB

    for h in range(N_HEADS):
        lo, hi = h * HEAD_DIM, (h + 1) * HEAD_DIM
        q = gla_ref[:, lo:hi] * (HEAD_DIM ** -0.5)
        k = jnp.where(valid, gla_ref[:, MIX_HALF + lo:MIX_HALF + hi], 0.0)
        v = gla_ref[:, 2 * MIX_HALF + lo:2 * MIX_HALF + hi]
        gate = gla_ref[:, 3 * MIX_HALF + lo:3 * MIX_HALF + hi]
        bc = b_cum[:, lo:hi]
        rr = r_ref[:, lo:hi]
        q_loc = q * jnp.exp(bc - rr)
        k_loc = k * jnp.exp(rr - bc)
        attn = jnp.where(same_sb & (ri >= ci), _dot_nt(q_loc.astype(BF16), k_loc.astype(BF16)), 0.0)
        if nsb > 1:
            qs, ks = [], []
            for j in range(nsb - 1):
                edge = b_cum[(j + 1) * GLA_SUB - 1:(j + 1) * GLA_SUB, lo:hi]
                qs.append(jnp.where(row_sb > j, q * jnp.exp(jnp.minimum(bc - edge, 0.0)), 0.0))
                ks.append(jnp.where(row_sb == j, k * jnp.exp(jnp.minimum(edge - bc, 0.0)), 0.0))
            attn = attn + _dot_nt(jnp.concatenate(qs, axis=1).astype(BF16),
                                  jnp.concatenate(ks, axis=1).astype(BF16))
        st = st_ref[h]
        v_bf = v.astype(BF16)
        o = _dot(attn.astype(BF16), v_bf) + _dot_nt((q * jnp.exp(bc)).astype(BF16), st.astype(BF16))
        b_last = b_cum[c - 1:c, lo:hi]
        k_dec = k * jnp.exp(b_last - bc)
        st_ref[h] = st * jnp.exp(b_last) + _dot_tn(v_bf, k_dec.astype(BF16))
        o = o * lax.rsqrt(jnp.mean(o * o, axis=-1, keepdims=True) + EPS)
        o_ref[:, lo:hi] = o * wnorm_ref[...] * _silu(gate)

    @pl.when(i == pl.num_programs(1) - 1)
    def _():
        for h in range(N_HEADS):
            s_out_ref[0, h] = st_ref[h].T


def _gla_mixer(gla, glog, s0, wnorm, *, nb, nchunks, chunk, l_real):
    rows = lambda w: pl.BlockSpec((chunk, w), lambda b, i: (b * nchunks + i, 0))
    state = pl.BlockSpec((1, N_HEADS, HEAD_DIM, HEAD_DIM), lambda b, i: (b, 0, 0, 0))
    return pl.pallas_call(
        functools.partial(_gla_kernel, chunk=chunk, l_real=l_real),
        grid=(nb, nchunks),
        in_specs=[rows(4 * MIX_HALF), rows(MIX_HALF), state,
                  pl.BlockSpec((1, HEAD_DIM), lambda b, i: (0, 0))],
        out_specs=[rows(MIX_HALF), state],
        out_shape=[jax.ShapeDtypeStruct((nb * nchunks * chunk, MIX_HALF), F32),
                   jax.ShapeDtypeStruct((nb, N_HEADS, HEAD_DIM, HEAD_DIM), F32)],
        scratch_shapes=[pltpu.VMEM((N_HEADS, HEAD_DIM, HEAD_DIM), F32)],
        compiler_params=pltpu.CompilerParams(dimension_semantics=("arbitrary", "arbitrary"),
                                             vmem_limit_bytes=VMEM_LIMIT),
        name="gla_mixer",
    )(gla, glog, s0, wnorm)


def _out_kernel(x_ref, odn_ref, ogla_ref, gt1_ref, sh2_ref, sc2_ref, gt2_ref, wo_ref, wn2_ref,
                wup_ref, wdown_ref, wnf_ref, y_ref):
    mix = (_dot(odn_ref[...].astype(BF16), wo_ref[:MIX_HALF, :])
           + _dot(ogla_ref[...].astype(BF16), wo_ref[MIX_HALF:, :]))
    x1 = x_ref[...] + gt1_ref[0] * mix
    h = _rms(x1, wn2_ref[...]) * (1.0 + sc2_ref[0]) + sh2_ref[0]
    up = _dot(h.astype(BF16), wup_ref[...])
    act = jnp.square(jnp.maximum(up, 0.0)).astype(BF16)
    x2 = x1 + gt2_ref[0] * _dot(act, wdown_ref[...])
    y_ref[...] = _rms(x2, wnf_ref[...])


def _out_mlp(x2d, odn, ogla, mod3, tiles_per_mod, tm, wo, wn2, wup, wdown, wnf):
    t_rows = x2d.shape[0]
    r = mod3.shape[1]
    row = lambda w: pl.BlockSpec((tm, w), lambda i: (i, 0))
    const = lambda shape: pl.BlockSpec(shape, lambda i: (0,) * len(shape), pipeline_mode=pl.Buffered(1))
    modspec = lambda j: pl.BlockSpec((1, r, D_MODEL), lambda i: (i // tiles_per_mod, 0, j))
    return pl.pallas_call(
        _out_kernel,
        grid=(t_rows // tm,),
        in_specs=[row(D_MODEL), row(MIX_HALF), row(MIX_HALF),
                  modspec(2), modspec(3), modspec(4), modspec(5),
                  const((D_MODEL, D_MODEL)), const((1, D_MODEL)), const((D_MODEL, D_FF)),
                  const((D_FF, D_MODEL)), const((1, D_MODEL))],
        out_specs=row(D_MODEL),
        out_shape=jax.ShapeDtypeStruct((t_rows, D_MODEL), F32),
        compiler_params=pltpu.CompilerParams(vmem_limit_bytes=VMEM_LIMIT),
        name="out_mlp",
    )(x2d, odn, ogla, mod3, mod3, mod3, mod3, wo, wn2, wup, wdown, wnf)


def _pad_rows(t, nb, l, lp):
    w = t.shape[-1]
    return jnp.pad(t.reshape(nb, l, w), ((0, 0), (0, lp - l), (0, 0))).reshape(nb * lp, w)


def _trunk(x, mod, conv0, s_dn0, s_gla0, wts, *, tm, chunk_dn, chunk_gla):
    (wn1, wmain, wsmall, wg2, bg, wconv8, alog, dtb, wdn, wgn, wo, wn2, wup, wdown, wnf) = wts
    nb, l, _ = x.shape
    x2d = x.reshape(nb * l, D_MODEL)
    if l % tm == 0:
        mod3, tiles_per_mod = mod.reshape(nb, 1, -1), l // tm
    else:
        mod3, tiles_per_mod = jnp.repeat(mod, l, axis=0).reshape(nb * l // tm, tm, -1), 1
    qkv, z, gla, small, glog = _inproj(x2d, mod3, tiles_per_mod, tm, wn1, wmain, wsmall, wg2, bg)
    conv_new = jnp.concatenate([conv0[:, SUBLANES - (CONV_W - 1):], qkv.reshape(nb, l, QKV_W)],
                               axis=1)[:, -(CONV_W - 1):]

    def padded(t, c):
        lp = -(-l // c) * c
        return (t if lp == l else _pad_rows(t, nb, l, lp)), lp // c

    qkv_p, n_dn = padded(qkv, chunk_dn)
    small_p, _ = padded(small, chunk_dn)
    z_p, _ = padded(z, chunk_dn)
    o_dn, s_dn = _dn_mixer(qkv_p, small_p, z_p, conv0, s_dn0, wconv8, alog, dtb, wdn,
                           nb=nb, nchunks=n_dn, chunk=chunk_dn, l_real=l)
    gla_p, n_gla = padded(gla, chunk_gla)
    glog_p, _ = padded(glog, chunk_gla)
    o_gla, s_gla = _gla_mixer(gla_p, glog_p, s_gla0, wgn, nb=nb, nchunks=n_gla, chunk=chunk_gla, l_real=l)
    if n_dn * chunk_dn != l:
        o_dn = o_dn.reshape(nb, n_dn * chunk_dn, MIX_HALF)[:, :l].reshape(nb * l, MIX_HALF)
    if n_gla * chunk_gla != l:
        o_gla = o_gla.reshape(nb, n_gla * chunk_gla, MIX_HALF)[:, :l].reshape(nb * l, MIX_HALF)
    y = _out_mlp(x2d, o_dn, o_gla, mod3, tiles_per_mod, tm, wo, wn2, wup, wdown, wnf)
    return y.reshape(nb, l, D_MODEL), conv_new[None], s_dn[None], s_gla[None]


def kernel(x_prompt, x_sample, state_dn_conv, state_dn, state_gla, c_prompt, c_sample, w_ada, b_ada, w_norm1, w_in, w_conv, dn_a_log, dn_dt_bias, w_dn_norm, w_gla_g2, b_gla_g, w_gla_norm, w_o, w_norm2, w_up, w_down, w_norm_f):
    nbp = x_prompt.shape[0]
    nbs = x_sample.shape[0]
    wi = w_in[0]
    o_a = QKV_W + MIX_HALF
    o_gq = o_a + 2 * N_HEADS
    o_gg = o_gq + 4 * MIX_HALF
    wmain = jnp.concatenate([wi[:, :o_a], wi[:, o_gq:o_gg]], axis=1).astype(BF16)
    wsmall = jnp.concatenate([wi[:, o_a:o_gq], wi[:, o_gg:],
                              jnp.zeros((D_MODEL, SMALL_COLS - 2 * N_HEADS - GATE_RANK), F32)], axis=1).astype(BF16)
    wg2 = jnp.zeros((SMALL_COLS, MIX_HALF), F32).at[2 * N_HEADS:2 * N_HEADS + GATE_RANK].set(w_gla_g2[0]).astype(BF16)
    bg = b_gla_g[0][None]
    wconv8 = jnp.pad(w_conv[0], ((0, SUBLANES - CONV_W), (0, 0)))
    alog = jnp.pad(dn_a_log[0], (0, SMALL_COLS - N_HEADS))[None]
    dtb = jnp.pad(dn_dt_bias[0], (0, SMALL_COLS - N_HEADS))[None]
    wts = (w_norm1[0][None], wmain, wsmall, wg2, bg, wconv8, alog, dtb, w_dn_norm[0][None],
           w_gla_norm[0][None], w_o[0].astype(BF16), w_norm2[0][None], w_up[0].astype(BF16),
           w_down[0].astype(BF16), w_norm_f[None])

    mod = _adaln(jnp.concatenate([c_prompt, c_sample], axis=0), w_ada[0], b_ada[0][None])

    conv0_p = jnp.zeros((nbp, SUBLANES, QKV_W), F32)
    zeros_state = jnp.zeros((nbp, N_HEADS, HEAD_DIM, HEAD_DIM), F32)
    y_p, conv_p, dn_p, gla_p = _trunk(x_prompt, mod[:nbp], conv0_p, zeros_state, zeros_state, wts,
                                      tm=256, chunk_dn=64, chunk_gla=64)
    conv0_s = jnp.pad(state_dn_conv[0], ((0, 0), (SUBLANES - (CONV_W - 1), 0), (0, 0)))
    y_s, conv_s, dn_s, gla_s = _trunk(x_sample, mod[nbp:], conv0_s, state_dn[0], state_gla[0], wts,
                                      tm=128, chunk_dn=16, chunk_gla=16)
    return (y_p, y_s, conv_p, dn_p, gla_p, conv_s, dn_s, gla_s)
```

```python
import functools

import jax
import jax.numpy as jnp
from jax import lax
from jax.experimental import pallas as pl
from jax.experimental.pallas import tpu as pltpu

F32 = jnp.float32
BF16 = jnp.bfloat16

D_MODEL = 1024
N_HEADS = 4
HEAD_DIM = 128
MIX_HALF = N_HEADS * HEAD_DIM
QKV_W = 3 * MIX_HALF
GATE_RANK = 16
GATE_NORM = 16.0
CONV_W = 4
D_FF = 4 * D_MODEL
EPS = 1e-6
MAIN_COLS = 4096
SMALL_COLS = 128
GLA_SUB = 16
SUBLANES = 8
SEQ_PER_STEP = 4

VMEM_LIMIT = 56 * 1024 * 1024


def _dot(a, b):
    return jnp.dot(a, b, preferred_element_type=F32)


def _dot_nt(a, b):
    return lax.dot_general(a, b, (((1,), (1,)), ((), ())), preferred_element_type=F32)


def _dot_tn(a, b):
    return lax.dot_general(a, b, (((0,), (0,)), ((), ())), preferred_element_type=F32)


def _bf(x):
    return x.astype(BF16)


def _sigmoid(x):
    return 1.0 / (1.0 + jnp.exp(-x))


def _silu(x):
    return x * _sigmoid(x)


def _softplus(x):
    return jnp.maximum(x, 0.0) + jnp.log(1.0 + jnp.exp(-jnp.abs(x)))


def _rms(x, w):
    return x * lax.rsqrt(jnp.mean(x * x, axis=-1, keepdims=True) + EPS) * w


def _split3(x):
    hi = _bf(x)
    r1 = x - hi.astype(F32)
    mid = _bf(r1)
    lo = _bf(r1 - mid.astype(F32))
    return jnp.concatenate([hi, mid, lo], axis=0)


def _prefix_sums(masks, x):
    lhs = jnp.concatenate([_bf(jnp.concatenate([m, m, m], axis=1)) for m in masks], axis=0)
    return _dot(lhs, _split3(x))


def _adaln_kernel(c_ref, w_ref, b_ref, o_ref):
    cs = _bf(_silu(c_ref[...]))
    o_ref[...] = _dot(cs, _bf(w_ref[...])) + b_ref[...]


def _adaln(c, w_ada, b_ada):
    n = c.shape[0]
    nblk = w_ada.shape[1] // D_MODEL
    return pl.pallas_call(
        _adaln_kernel,
        grid=(nblk,),
        in_specs=[
            pl.BlockSpec((n, D_MODEL), lambda j: (0, 0)),
            pl.BlockSpec((D_MODEL, D_MODEL), lambda j: (0, j)),
            pl.BlockSpec((1, D_MODEL), lambda j: (0, j)),
        ],
        out_specs=pl.BlockSpec((n, D_MODEL), lambda j: (0, j)),
        out_shape=jax.ShapeDtypeStruct((n, w_ada.shape[1]), F32),
        compiler_params=pltpu.CompilerParams(vmem_limit_bytes=VMEM_LIMIT),
        name="adaln",
    )(c, w_ada, b_ada)


def _inproj_kernel(x_ref, sh_ref, sc_ref, wn_ref, wmain_ref, wsmall_ref, wg2_ref, bg_ref,
                   qkv_ref, z_ref, gla_ref, small_ref, glog_ref):
    h = _rms(x_ref[...], wn_ref[...]) * (1.0 + sc_ref[0]) + sh_ref[0]
    hb = _bf(h)
    main = _dot(hb, wmain_ref[...])
    qkv_ref[...] = main[:, :QKV_W]
    z_ref[...] = main[:, QKV_W:QKV_W + MIX_HALF]
    gla_ref[...] = main[:, QKV_W + MIX_HALF:]
    small = _dot(hb, wsmall_ref[...])
    small_ref[...] = small
    pre = _dot(_bf(small), wg2_ref[...]) + bg_ref[...]
    log_sig = jnp.minimum(pre, 0.0) - jnp.log(1.0 + jnp.exp(-jnp.abs(pre)))
    glog_ref[...] = log_sig / GATE_NORM


def _inproj(x2d, mod3, tiles_per_mod, tm, wn1, wmain, wsmall, wg2, bg):
    t_rows = x2d.shape[0]
    r = mod3.shape[1]
    grid = (t_rows // tm,)
    row = lambda w: pl.BlockSpec((tm, w), lambda i: (i, 0))
    const = lambda shape: pl.BlockSpec(shape, lambda i: (0,) * len(shape), pipeline_mode=pl.Buffered(1))
    modspec = lambda j: pl.BlockSpec((1, r, D_MODEL), lambda i: (i // tiles_per_mod, 0, j))
    return pl.pallas_call(
        _inproj_kernel,
        grid=grid,
        in_specs=[row(D_MODEL), modspec(0), modspec(1), const((1, D_MODEL)),
                  const((D_MODEL, MAIN_COLS)), const((D_MODEL, SMALL_COLS)),
                  const((SMALL_COLS, MIX_HALF)), const((1, MIX_HALF))],
        out_specs=[row(QKV_W), row(MIX_HALF), row(4 * MIX_HALF), row(SMALL_COLS), row(MIX_HALF)],
        out_shape=[jax.ShapeDtypeStruct((t_rows, w), F32)
                   for w in (QKV_W, MIX_HALF, 4 * MIX_HALF, SMALL_COLS, MIX_HALF)],
        compiler_params=pltpu.CompilerParams(vmem_limit_bytes=VMEM_LIMIT),
        name="inproj",
    )(x2d, mod3, mod3, wn1, wmain, wsmall, wg2, bg)


def _dn_kernel(qkv_ref, small_ref, z_ref, conv0_ref, s0_ref, wconv_ref, alog_ref, dtb_ref, wnorm_ref,
               o_ref, s_out_ref, tail_ref, s_ref, *, chunk, n_valid, n_seq):
    c = chunk
    i = pl.program_id(1)

    @pl.when(i == 0)
    def _():
        tail_ref[...] = conv0_ref[...]
        s_ref[...] = s0_ref[...]

    ri = lax.broadcasted_iota(jnp.int32, (c, c), 0)
    ci = lax.broadcasted_iota(jnp.int32, (c, c), 1)
    incl = ri >= ci
    strict = ri > ci
    eye = jnp.where(ri == ci, 1.0, 0.0)
    ltri = jnp.where(incl, 1.0, 0.0)
    levels = []
    s = 2
    while s < c:
        levels.append(((ri // (2 * s)) == (ci // (2 * s))) & ((ri // s) != (ci // s)))
        s *= 2
    pair = (ri // 2) == (ci // 2)
    padded = n_valid < c
    valid = lax.broadcasted_iota(jnp.int32, (c, 1), 0) < n_valid
    w = wconv_ref[...]

    chains = [(gi, h) for gi in range(n_seq) for h in range(N_HEADS)]
    acts, g_cums, g_cum_ts, eg_alls, beta_alls = [], [], [], [], []
    for gi in range(n_seq):
        u = qkv_ref[gi]
        ext = jnp.concatenate([tail_ref[gi], u], axis=0)
        acc = u * w[CONV_W - 1:CONV_W]
        for sh in range(1, CONV_W):
            shifted = pltpu.roll(ext, sh, 0)[SUBLANES:SUBLANES + c]
            acc = acc + shifted * w[CONV_W - 1 - sh:CONV_W - sh]
        tail_ref[gi] = ext[c:c + SUBLANES]
        acts.append(_silu(acc))
        small = small_ref[gi]
        g_all = -jnp.exp(alog_ref[...]) * _softplus(small + dtb_ref[...])
        beta_all = _sigmoid(small)
        if padded:
            g_all = jnp.where(valid, g_all, 0.0)
            beta_all = jnp.where(valid, beta_all, 0.0)
        beta_alls.append(beta_all)
        g_cums.append(_prefix_sums([ltri], g_all))
    for gi in range(n_seq):
        g_cum_ts.append(g_cums[gi].T)
        eg_alls.append(jnp.exp(g_cums[gi]))

    qs, ks, kbs, vbs, gcs, egs, es = [], [], [], [], [], [], []
    for gi, h in chains:
        lo, hi = h * HEAD_DIM, (h + 1) * HEAD_DIM
        act = acts[gi]
        gc = g_cums[gi][:, h:h + 1]
        gr = g_cum_ts[gi][h:h + 1, :]
        beta = beta_alls[gi][:, N_HEADS + h:N_HEADS + h + 1]
        q = act[:, lo:hi]
        k = act[:, MIX_HALF + lo:MIX_HALF + hi]
        q = q * lax.rsqrt(jnp.sum(q * q, axis=-1, keepdims=True) + EPS) * (HEAD_DIM ** -0.5)
        k = k * lax.rsqrt(jnp.sum(k * k, axis=-1, keepdims=True) + EPS)
        if padded:
            k = jnp.where(valid, k, 0.0)
        qs.append(q)
        ks.append(k)
        kbs.append(k * beta)
        vbs.append(act[:, 2 * MIX_HALF + lo:2 * MIX_HALF + hi] * beta)
        gcs.append(gc)
        egs.append(eg_alls[gi][:, h:h + 1])
        es.append(jnp.exp(jnp.minimum(gc - gr, 0.0)))
    n = len(chains)
    qks = [_dot_nt(_bf(jnp.concatenate([kbs[j], qs[j]], axis=0)), _bf(ks[j])) for j in range(n)]
    a_s = [jnp.where(strict, qks[j][:c] * es[j], 0.0) for j in range(n)]
    attns = [_bf(jnp.where(incl, qks[j][c:] * es[j], 0.0)) for j in range(n)]
    ts = [eye - jnp.where(pair, a_s[j], 0.0) for j in range(n)]
    for lmask in levels:
        t_bfs = [_bf(t) for t in ts]
        tl = [_bf(_dot(t_bfs[j], _bf(jnp.where(lmask, a_s[j], 0.0)))) for j in range(n)]
        ts = [ts[j] - _dot(tl[j], t_bfs[j]) for j in range(n)]
    rhss = [jnp.concatenate([vbs[j], kbs[j] * egs[j]], axis=1) for j in range(n)]
    sols = [rhss[j] + _dot(_bf(ts[j] - eye), _bf(rhss[j])) for j in range(n)]
    s_old = [s_ref[gi, h] for gi, h in chains]
    sus = [_dot(_bf(jnp.concatenate([sols[j][:, HEAD_DIM:], qs[j] * egs[j]], axis=0)), _bf(s_old[j]))
           for j in range(n)]
    u_bfs = [_bf(sols[j][:, :HEAD_DIM] - sus[j][:c]) for j in range(n)]
    os_ = [sus[j][c:] + _dot(attns[j], u_bfs[j]) for j in range(n)]
    for j, (gi, h) in enumerate(chains):
        g_last = g_cums[gi][c - 1:c, h:h + 1]
        k_dec = ks[j] * jnp.exp(g_last - gcs[j])
        s_ref[gi, h] = s_old[j] * jnp.exp(g_last) + _dot_tn(_bf(k_dec), u_bfs[j])
    for j, (gi, h) in enumerate(chains):
        lo, hi = h * HEAD_DIM, (h + 1) * HEAD_DIM
        o = os_[j] * lax.rsqrt(jnp.mean(os_[j] * os_[j], axis=-1, keepdims=True) + EPS)
        o_ref[gi, :, lo:hi] = o * wnorm_ref[...] * _silu(z_ref[gi, :, lo:hi])

    @pl.when(i == pl.num_programs(1) - 1)
    def _():
        s_out_ref[...] = s_ref[...]


def _dn_mixer(qkv, small, z, conv0, s0, wconv8, alog, dtb, wnorm, *, chunk, n_valid):
    nb, lp, _ = qkv.shape
    g = SEQ_PER_STEP
    rows = lambda w: pl.BlockSpec((g, chunk, w), lambda b, i: (b, i, 0))
    const = lambda shape: pl.BlockSpec(shape, lambda b, i: (0,) * len(shape))
    state = pl.BlockSpec((g, N_HEADS, HEAD_DIM, HEAD_DIM), lambda b, i: (b, 0, 0, 0))
    return pl.pallas_call(
        functools.partial(_dn_kernel, chunk=chunk, n_valid=n_valid, n_seq=g),
        grid=(nb // g, lp // chunk),
        in_specs=[rows(QKV_W), rows(SMALL_COLS), rows(MIX_HALF),
                  pl.BlockSpec((g, SUBLANES, QKV_W), lambda b, i: (b, 0, 0)), state,
                  const((SUBLANES, QKV_W)), const((1, SMALL_COLS)), const((1, SMALL_COLS)),
                  const((1, HEAD_DIM))],
        out_specs=[rows(MIX_HALF), state],
        out_shape=[jax.ShapeDtypeStruct((nb, lp, MIX_HALF), F32),
                   jax.ShapeDtypeStruct((nb, N_HEADS, HEAD_DIM, HEAD_DIM), F32)],
        scratch_shapes=[pltpu.VMEM((g, SUBLANES, QKV_W), F32),
                        pltpu.VMEM((g, N_HEADS, HEAD_DIM, HEAD_DIM), F32)],
        compiler_params=pltpu.CompilerParams(dimension_semantics=("arbitrary", "arbitrary"),
                                             vmem_limit_bytes=VMEM_LIMIT),
        name="dn_mixer",
    )(qkv, small, z, conv0, s0, wconv8, alog, dtb, wnorm)


def _gla_kernel(gla_ref, glog_ref, s0_ref, wnorm_ref, o_ref, s_out_ref, st_ref, *, chunk, n_valid, n_seq):
    c = chunk
    nsb = c // GLA_SUB
    i = pl.program_id(1)

    @pl.when(i == 0)
    def _():
        for gi in range(n_seq):
            for h in range(N_HEADS):
                st_ref[gi, h] = s0_ref[gi, h].T

    padded = n_valid < c
    rows = lax.broadcasted_iota(jnp.int32, (c, 1), 0)
    valid = rows < n_valid
    ri = lax.broadcasted_iota(jnp.int32, (c, c), 0)
    ci = lax.broadcasted_iota(jnp.int32, (c, c), 1)
    ltri = jnp.where(ri >= ci, 1.0, 0.0)
    before_sb = jnp.where(ci < (ri // GLA_SUB) * GLA_SUB, 1.0, 0.0)
    diag_mask = ((ri // GLA_SUB) == (ci // GLA_SUB)) & (ri >= ci)
    row_sb = rows // GLA_SUB

    chains = [(gi, h) for gi in range(n_seq) for h in range(N_HEADS)]
    n = len(chains)
    b_cums, r_refs = [], []
    for gi in range(n_seq):
        glog = glog_ref[gi]
        if padded:
            glog = jnp.where(valid, glog, 0.0)
        sums = _prefix_sums([ltri, before_sb], glog)
        b_cums.append(sums[:c])
        r_refs.append(sums[c:])
    qs, ks, v_bfs, bcs = [], [], [], []
    for gi, h in chains:
        lo, hi = h * HEAD_DIM, (h + 1) * HEAD_DIM
        qs.append(gla_ref[gi, :, lo:hi] * (HEAD_DIM ** -0.5))
        k = gla_ref[gi, :, MIX_HALF + lo:MIX_HALF + hi]
        ks.append(jnp.where(valid, k, 0.0) if padded else k)
        v_bfs.append(_bf(gla_ref[gi, :, 2 * MIX_HALF + lo:2 * MIX_HALF + hi]))
        bcs.append(b_cums[gi][:, lo:hi])
    diffs = [bcs[j] - r_refs[gi][:, h * HEAD_DIM:(h + 1) * HEAD_DIM] for j, (gi, h) in enumerate(chains)]
    attns = [_dot_nt(_bf(qs[j] * jnp.exp(diffs[j])), _bf(ks[j] * jnp.exp(-diffs[j]))) for j in range(n)]
    if nsb > 1:
        offs = []
        for j in range(n):
            qj, kj = [], []
            for sb in range(nsb - 1):
                edge = bcs[j][(sb + 1) * GLA_SUB - 1:(sb + 1) * GLA_SUB]
                qj.append(jnp.where(row_sb > sb, qs[j] * jnp.exp(jnp.minimum(bcs[j] - edge, 0.0)), 0.0))
                kj.append(jnp.where(row_sb == sb, ks[j] * jnp.exp(jnp.minimum(edge - bcs[j], 0.0)), 0.0))
            offs.append(_dot_nt(_bf(jnp.concatenate(qj, axis=1)), _bf(jnp.concatenate(kj, axis=1))))
        attns = [_bf(jnp.where(diag_mask, attns[j], offs[j])) for j in range(n)]
    else:
        attns = [_bf(jnp.where(diag_mask, attns[j], 0.0)) for j in range(n)]
    sts = [st_ref[gi, h] for gi, h in chains]
    os_ = [_dot(attns[j], v_bfs[j]) + _dot_nt(_bf(qs[j] * jnp.exp(bcs[j])), _bf(sts[j])) for j in range(n)]
    for j, (gi, h) in enumerate(chains):
        b_last = bcs[j][c - 1:c]
        k_dec = ks[j] * jnp.exp(b_last - bcs[j])
        st_ref[gi, h] = sts[j] * jnp.exp(b_last) + _dot_tn(v_bfs[j], _bf(k_dec))
    for j, (gi, h) in enumerate(chains):
        lo, hi = h * HEAD_DIM, (h + 1) * HEAD_DIM
        gate = gla_ref[gi, :, 3 * MIX_HALF + lo:3 * MIX_HALF + hi]
        o = os_[j] * lax.rsqrt(jnp.mean(os_[j] * os_[j], axis=-1, keepdims=True) + EPS)
        o_ref[gi, :, lo:hi] = o * wnorm_ref[...] * _silu(gate)

    @pl.when(i == pl.num_programs(1) - 1)
    def _():
        for gi in range(n_seq):
            for h in range(N_HEADS):
                s_out_ref[gi, h] = st_ref[gi, h].T


def _gla_mixer(gla, glog, s0, wnorm, *, chunk, n_valid):
    nb, lp, _ = gla.shape
    g = SEQ_PER_STEP
    rows = lambda w: pl.BlockSpec((g, chunk, w), lambda b, i: (b, i, 0))
    state = pl.BlockSpec((g, N_HEADS, HEAD_DIM, HEAD_DIM), lambda b, i: (b, 0, 0, 0))
    return pl.pallas_call(
        functools.partial(_gla_kernel, chunk=chunk, n_valid=n_valid, n_seq=g),
        grid=(nb // g, lp // chunk),
        in_specs=[rows(4 * MIX_HALF), rows(MIX_HALF), state,
                  pl.BlockSpec((1, HEAD_DIM), lambda b, i: (0, 0))],
        out_specs=[rows(MIX_HALF), state],
        out_shape=[jax.ShapeDtypeStruct((nb, lp, MIX_HALF), F32),
                   jax.ShapeDtypeStruct((nb, N_HEADS, HEAD_DIM, HEAD_DIM), F32)],
        scratch_shapes=[pltpu.VMEM((g, N_HEADS, HEAD_DIM, HEAD_DIM), F32)],
        compiler_params=pltpu.CompilerParams(dimension_semantics=("arbitrary", "arbitrary"),
                                             vmem_limit_bytes=VMEM_LIMIT),
        name="gla_mixer",
    )(gla, glog, s0, wnorm)


def _out_kernel(x_ref, odn_ref, ogla_ref, gt1_ref, sh2_ref, sc2_ref, gt2_ref, wo_ref, wn2_ref,
                wup_ref, wdown_ref, wnf_ref, y_ref):
    mix = (_dot(_bf(odn_ref[...]), wo_ref[:MIX_HALF, :])
           + _dot(_bf(ogla_ref[...]), wo_ref[MIX_HALF:, :]))
    x1 = x_ref[...] + gt1_ref[0] * mix
    h = _rms(x1, wn2_ref[...]) * (1.0 + sc2_ref[0]) + sh2_ref[0]
    up = _dot(_bf(h), wup_ref[...])
    act = _bf(jnp.square(jnp.maximum(up, 0.0)))
    x2 = x1 + gt2_ref[0] * _dot(act, wdown_ref[...])
    y_ref[...] = _rms(x2, wnf_ref[...])


def _out_mlp(x2d, odn, ogla, mod3, tiles_per_mod, tm, wo, wn2, wup, wdown, wnf):
    t_rows = x2d.shape[0]
    r = mod3.shape[1]
    row = lambda w: pl.BlockSpec((tm, w), lambda i: (i, 0))
    const = lambda shape: pl.BlockSpec(shape, lambda i: (0,) * len(shape), pipeline_mode=pl.Buffered(1))
    modspec = lambda j: pl.BlockSpec((1, r, D_MODEL), lambda i: (i // tiles_per_mod, 0, j))
    return pl.pallas_call(
        _out_kernel,
        grid=(t_rows // tm,),
        in_specs=[row(D_MODEL), row(MIX_HALF), row(MIX_HALF),
                  modspec(2), modspec(3), modspec(4), modspec(5),
                  const((D_MODEL, D_MODEL)), const((1, D_MODEL)), const((D_MODEL, D_FF)),
                  const((D_FF, D_MODEL)), const((1, D_MODEL))],
        out_specs=row(D_MODEL),
        out_shape=jax.ShapeDtypeStruct((t_rows, D_MODEL), F32),
        compiler_params=pltpu.CompilerParams(vmem_limit_bytes=VMEM_LIMIT),
        name="out_mlp",
    )(x2d, odn, ogla, mod3, mod3, mod3, mod3, wo, wn2, wup, wdown, wnf)


def _trunk(x, mod, conv0, s_dn0, s_gla0, wts, *, tm, chunk_dn, chunk_gla):
    (wn1, wmain, wsmall, wg2, bg, wconv8, alog, dtb, wdn, wgn, wo, wn2, wup, wdown, wnf) = wts
    nb, l, _ = x.shape
    x2d = x.reshape(nb * l, D_MODEL)
    if l % tm == 0:
        mod3, tiles_per_mod = mod.reshape(nb, 1, -1), l // tm
    else:
        mod3, tiles_per_mod = jnp.repeat(mod, l, axis=0).reshape(nb * l // tm, tm, -1), 1
    qkv, z, gla, small, glog = _inproj(x2d, mod3, tiles_per_mod, tm, wn1, wmain, wsmall, wg2, bg)
    qkv = qkv.reshape(nb, l, QKV_W)
    conv_new = jnp.concatenate([conv0[:, SUBLANES - (CONV_W - 1):], qkv], axis=1)[:, -(CONV_W - 1):]

    def seqs(t, c):
        t = t.reshape(nb, l, -1)
        lp = -(-l // c) * c
        return t if lp == l else jnp.pad(t, ((0, 0), (0, lp - l), (0, 0)))

    nv_dn = chunk_dn if l % chunk_dn == 0 else l % chunk_dn
    nv_gla = chunk_gla if l % chunk_gla == 0 else l % chunk_gla
    o_dn, s_dn = _dn_mixer(seqs(qkv, chunk_dn), seqs(small, chunk_dn), seqs(z, chunk_dn), conv0, s_dn0,
                           wconv8, alog, dtb, wdn, chunk=chunk_dn, n_valid=nv_dn)
    o_gla, s_gla = _gla_mixer(seqs(gla, chunk_gla), seqs(glog, chunk_gla), s_gla0, wgn,
                              chunk=chunk_gla, n_valid=nv_gla)
    o_dn = o_dn[:, :l].reshape(nb * l, MIX_HALF)
    o_gla = o_gla[:, :l].reshape(nb * l, MIX_HALF)
    y = _out_mlp(x2d, o_dn, o_gla, mod3, tiles_per_mod, tm, wo, wn2, wup, wdown, wnf)
    return y.reshape(nb, l, D_MODEL), conv_new[None], s_dn[None], s_gla[None]


def kernel(x_prompt, x_sample, state_dn_conv, state_dn, state_gla, c_prompt, c_sample, w_ada, b_ada, w_norm1, w_in, w_conv, dn_a_log, dn_dt_bias, w_dn_norm, w_gla_g2, b_gla_g, w_gla_norm, w_o, w_norm2, w_up, w_down, w_norm_f):
    nbp = x_prompt.shape[0]
    wi = w_in[0]
    o_a = QKV_W + MIX_HALF
    o_gq = o_a + 2 * N_HEADS
    o_gg = o_gq + 4 * MIX_HALF
    wmain = _bf(jnp.concatenate([wi[:, :o_a], wi[:, o_gq:o_gg]], axis=1))
    wsmall = _bf(jnp.concatenate([wi[:, o_a:o_gq], wi[:, o_gg:],
                                  jnp.zeros((D_MODEL, SMALL_COLS - 2 * N_HEADS - GATE_RANK), F32)], axis=1))
    wg2 = _bf(jnp.zeros((SMALL_COLS, MIX_HALF), F32).at[2 * N_HEADS:2 * N_HEADS + GATE_RANK].set(w_gla_g2[0]))
    bg = b_gla_g[0][None]
    wconv8 = jnp.pad(w_conv[0], ((0, SUBLANES - CONV_W), (0, 0)))
    alog = jnp.pad(dn_a_log[0], (0, SMALL_COLS - N_HEADS))[None]
    dtb = jnp.pad(dn_dt_bias[0], (0, SMALL_COLS - N_HEADS))[None]
    wts = (w_norm1[0][None], wmain, wsmall, wg2, bg, wconv8, alog, dtb, w_dn_norm[0][None],
           w_gla_norm[0][None], _bf(w_o[0]), w_norm2[0][None], _bf(w_up[0]),
           _bf(w_down[0]), w_norm_f[None])

    mod = _adaln(jnp.concatenate([c_prompt, c_sample], axis=0), w_ada[0], b_ada[0][None])

    conv0_p = jnp.zeros((nbp, SUBLANES, QKV_W), F32)
    zeros_state = jnp.zeros((nbp, N_HEADS, HEAD_DIM, HEAD_DIM), F32)
    y_p, conv_p, dn_p, gla_p = _trunk(x_prompt, mod[:nbp], conv0_p, zeros_state, zeros_state, wts,
                                      tm=256, chunk_dn=64, chunk_gla=64)
    conv0_s = jnp.pad(state_dn_conv[0], ((0, 0), (SUBLANES - (CONV_W - 1), 0), (0, 0)))
    y_s, conv_s, dn_s, gla_s = _trunk(x_sample, mod[nbp:], conv0_s, state_dn[0], state_gla[0], wts,
                                      tm=128, chunk_dn=16, chunk_gla=16)
    return (y_p, y_s, conv_p, dn_p, gla_p, conv_s, dn_s, gla_s)
```
